```python
import math
import jax, jax.numpy as jnp
from jax import lax
import numpy as np

D_MODEL = 1024
BATCH = 4
SEQ = 8192
DEPTH = 1

ATT_HEADS = 8
HEAD_DIM = 64
ATT_WIDTH = ATT_HEADS * HEAD_DIM
MOBA_BLOCK = 256
MOBA_TOPK = 3
Q_CHUNK = 64
ATT_SCALE = 1.0 / math.sqrt(HEAD_DIM)
SSM_WIDTH = D_MODEL // 2
SSM_GROUP = 16
SSM_GROUPS = SSM_WIDTH // SSM_GROUP
SSM_STATE = 64
DT_MIN = 1e-3
DT_MAX = 1e-1
D_FF = 2816
N_ADA = 9
EPS = 1e-6
NEG = -1e30
IN_SPLITS = [ATT_WIDTH, 2 * ATT_WIDTH, 3 * ATT_WIDTH, 3 * ATT_WIDTH + SSM_WIDTH,
             3 * ATT_WIDTH + SSM_WIDTH + D_MODEL]
IN_COLS = 3 * ATT_WIDTH + SSM_WIDTH + 2 * D_MODEL

kernel_name = "hybrid_moba_s5_macaron_adaln"


def rmsnorm(x, g):
    xf = x.astype(jnp.float32)
    y = xf * lax.rsqrt(jnp.mean(xf * xf, axis=-1, keepdims=True) + EPS)
    return (y * g.astype(jnp.float32)).astype(x.dtype)


def modulate(h, shift, scale):
    return h * (1.0 + scale[:, None, :]) + shift[:, None, :]


def swiglu(h, w_gu, w_down):
    g, u = jnp.split(h @ w_gu, 2, axis=-1)
    return (jax.nn.silu(g) * u) @ w_down


def alibi_slopes():
    return jnp.asarray([2.0 ** (-8.0 * (h + 1) / ATT_HEADS) for h in range(ATT_HEADS)],
                       dtype=jnp.float32)


def moba_attention(q, k, v):
    bsz, s = q.shape[0], q.shape[1]
    nb = -(-s // MOBA_BLOCK)
    s_pad = nb * MOBA_BLOCK
    n_sel = min(MOBA_TOPK, nb)
    pad = ((0, 0), (0, s_pad - s), (0, 0), (0, 0))
    q, k, v = [jnp.pad(t, pad).transpose(0, 2, 1, 3) for t in (q, k, v)]
    kb = k.reshape(bsz, ATT_HEADS, nb, MOBA_BLOCK, HEAD_DIM)
    vb = v.reshape(bsz, ATT_HEADS, nb, MOBA_BLOCK, HEAD_DIM)
    kmean = jnp.mean(kb.astype(jnp.float32), axis=3)
    slopes = alibi_slopes()[None, :, None, None]
    bi = jnp.arange(bsz)[:, None, None]
    hi = jnp.arange(ATT_HEADS)[None, :, None]
    blk_ids = jnp.arange(nb)
    key_off = jnp.arange(MOBA_BLOCK)

    def chunk(ci):
        q0 = ci * Q_CHUNK
        own = q0 // MOBA_BLOCK
        qc = lax.dynamic_slice_in_dim(q, q0, Q_CHUNK, axis=2)
        tq = (q0 + jnp.arange(Q_CHUNK)).astype(jnp.float32)
        gate = jnp.einsum('bhqd,bhnd->bhqn', qc.astype(jnp.float32), kmean)
        gate = jnp.where(blk_ids < own, gate, NEG)
        _, sel = lax.top_k(gate, n_sel)
        scores, vals = [], []
        for j in range(n_sel):
            idx = sel[..., j]
            kj = kb[bi, hi, idx]
            vj = vb[bi, hi, idx]
            sj = jnp.einsum('bhqd,bhqkd->bhqk', qc, kj).astype(jnp.float32) * ATT_SCALE
            kpos = (idx[..., None] * MOBA_BLOCK + key_off).astype(jnp.float32)
            sj = sj - slopes * (tq[:, None] - kpos)
            sj = jnp.where(j < own, sj, NEG)
            scores.append(sj)
            vals.append(vj)
        ko = lax.dynamic_slice_in_dim(k, own * MOBA_BLOCK, MOBA_BLOCK, axis=2)
        vo = lax.dynamic_slice_in_dim(v, own * MOBA_BLOCK, MOBA_BLOCK, axis=2)
        so = jnp.einsum('bhqd,bhkd->bhqk', qc, ko).astype(jnp.float32) * ATT_SCALE
        dist = tq[:, None] - (own * MOBA_BLOCK + key_off).astype(jnp.float32)[None, :]
        so = jnp.where(dist >= 0, so - slopes * dist, NEG)
        scores.append(so)
        p = jax.nn.softmax(jnp.concatenate(scores, axis=-1), axis=-1).astype(v.dtype)
        p = p.reshape(bsz, ATT_HEADS, Q_CHUNK, n_sel + 1, MOBA_BLOCK)
        out = jnp.einsum('bhqk,bhkd->bhqd', p[..., n_sel, :], vo)
        for j in range(n_sel):
            out = out + jnp.einsum('bhqk,bhqkd->bhqd', p[..., j, :], vals[j])
        return out

    out = lax.map(chunk, jnp.arange(s_pad // Q_CHUNK))
    out = out.transpose(1, 0, 3, 2, 4).reshape(bsz, s_pad, ATT_WIDTH)
    return out[:, :s]


def s5_mixer(u, lam_re, lam_im, log_dt, b_re, b_im, c_re, c_im, d_skip, w_glu, b_glu):
    bsz, s, _ = u.shape
    f32 = jnp.float32
    uf = u.astype(f32)
    ug = uf.reshape(bsz, s, SSM_GROUPS, SSM_GROUP)
    lr, li = lam_re.astype(f32), lam_im.astype(f32)
    dt = jnp.exp(log_dt.astype(f32))[:, None]
    mag = jnp.exp(lr * dt)
    ang = li * dt
    ab_re, ab_im = mag * jnp.cos(ang), mag * jnp.sin(ang)
    nr, ni = ab_re - 1.0, ab_im
    den = lr * lr + li * li
    f_re = (nr * lr + ni * li) / den
    f_im = (ni * lr - nr * li) / den
    br, bim = b_re.astype(f32), b_im.astype(f32)
    bb_re = f_re[..., None] * br - f_im[..., None] * bim
    bb_im = f_re[..., None] * bim + f_im[..., None] * br
    bu_re = jnp.einsum('bsgc,gpc->sbgp', ug, bb_re)
    bu_im = jnp.einsum('bsgc,gpc->sbgp', ug, bb_im)
    a_re = jnp.broadcast_to(ab_re, bu_re.shape)
    a_im = jnp.broadcast_to(ab_im, bu_im.shape)

    def combine(e1, e2):
        a1r, a1i, b1r, b1i = e1
        a2r, a2i, b2r, b2i = e2
        return (a2r * a1r - a2i * a1i,
                a2r * a1i + a2i * a1r,
                a2r * b1r - a2i * b1i + b2r,
                a2r * b1i + a2i * b1r + b2i)

    _, _, xr, xi = lax.associative_scan(combine, (a_re, a_im, bu_re, bu_im), axis=0)
    y = (jnp.einsum('sbgp,gcp->bsgc', xr, c_re.astype(f32))
         - jnp.einsum('sbgp,gcp->bsgc', xi, c_im.astype(f32)))
    y = y.reshape(bsz, s, SSM_WIDTH) + d_skip.astype(f32) * uf
    y = jax.nn.gelu(y)
    y = y * jax.nn.sigmoid(y @ w_glu.astype(f32) + b_glu.astype(f32))
    return y.astype(u.dtype)


def setup_inputs(seed: int = 0) -> dict:
    key = jax.random.key(seed)
    ks = jax.random.split(key, 32)
    n = lambda k, shape, s: jax.random.normal(k, shape, jnp.float32) * s
    L, D, G, P, C = DEPTH, D_MODEL, SSM_GROUPS, SSM_STATE, SSM_GROUP
    lam_im0 = jnp.pi * jnp.arange(P, dtype=jnp.float32)
    return {
        "x": n(ks[0], (BATCH, SEQ, D), 1.0),
        "c": n(ks[1], (BATCH, D), 1.0),
        "w_ada": n(ks[2], (L, D, N_ADA * D), D ** -0.5),
        "b_ada": n(ks[3], (L, N_ADA * D), 0.01),
        "norm_ffn1": 1.0 + n(ks[4], (L, D), 0.02),
        "w_ffn1_in": n(ks[5], (L, D, 2 * D_FF), D ** -0.5),
        "w_ffn1_out": n(ks[6], (L, D_FF, D), D_FF ** -0.5),
        "norm_mix": 1.0 + n(ks[7], (L, D), 0.02),
        "w_in": n(ks[8], (L, D, IN_COLS), D ** -0.5),
        "lam_re": -0.5 + n(ks[9], (L, G, P), 0.01),
        "lam_im": lam_im0 + n(ks[10], (L, G, P), 0.01),
        "log_dt": jax.random.uniform(ks[11], (L, G), jnp.float32,
                                     math.log(DT_MIN), math.log(DT_MAX)),
        "ssm_b_re": n(ks[12], (L, G, P, C), (2 * C) ** -0.5),
        "ssm_b_im": n(ks[13], (L, G, P, C), (2 * C) ** -0.5),
        "ssm_c_re": n(ks[14], (L, G, C, P), (2 * P) ** -0.5 * 4.0),
        "ssm_c_im": n(ks[15], (L, G, C, P), (2 * P) ** -0.5 * 4.0),
        "ssm_d": n(ks[16], (L, SSM_WIDTH), 1.0),
        "w_glu": n(ks[17], (L, SSM_WIDTH, SSM_WIDTH), SSM_WIDTH ** -0.5),
        "b_glu": n(ks[18], (L, SSM_WIDTH), 0.01),
        "w_br_att": n(ks[19], (L, ATT_WIDTH, D), ATT_WIDTH ** -0.5),
        "w_br_ssm": n(ks[20], (L, SSM_WIDTH, D), SSM_WIDTH ** -0.5),
        "w_out": n(ks[21], (L, D, D), D ** -0.5),
        "norm_ffn2": 1.0 + n(ks[22], (L, D), 0.02),
        "w_ffn2_in": n(ks[23], (L, D, 2 * D_FF), D ** -0.5),
        "w_ffn2_out": n(ks[24], (L, D_FF, D), D_FF ** -0.5),
        "norm_final": 1.0 + n(ks[25], (D,), 0.02),
    }


def reference(x, c, w_ada, b_ada, norm_ffn1, w_ffn1_in, w_ffn1_out, norm_mix, w_in,
              lam_re, lam_im, log_dt, ssm_b_re, ssm_b_im, ssm_c_re, ssm_c_im, ssm_d,
              w_glu, b_glu, w_br_att, w_br_ssm, w_out, norm_ffn2, w_ffn2_in, w_ffn2_out,
              norm_final):
    bsz, s, _ = x.shape
    for l in range(DEPTH):
        ada = jax.nn.silu(c) @ w_ada[l] + b_ada[l]
        sh1, sc1, g1, sh2, sc2, g2, sh3, sc3, g3 = jnp.split(ada, N_ADA, axis=-1)
        h = modulate(rmsnorm(x, norm_ffn1[l]), sh1, sc1)
        x = x + 0.5 * g1[:, None, :] * swiglu(h, w_ffn1_in[l], w_ffn1_out[l])
        h = modulate(rmsnorm(x, norm_mix[l]), sh2, sc2)
        q, k, v, u, ga, gs = jnp.split(h @ w_in[l], IN_SPLITS, axis=-1)
        hd = (bsz, s, ATT_HEADS, HEAD_DIM)
        y_att = moba_attention(q.reshape(hd), k.reshape(hd), v.reshape(hd))
        y_ssm = s5_mixer(u, lam_re[l], lam_im[l], log_dt[l], ssm_b_re[l], ssm_b_im[l],
                         ssm_c_re[l], ssm_c_im[l], ssm_d[l], w_glu[l], b_glu[l])
        merged = (jax.nn.sigmoid(ga) * (y_att @ w_br_att[l])
                  + jax.nn.sigmoid(gs) * (y_ssm @ w_br_ssm[l]))
        x = x + g2[:, None, :] * (merged @ w_out[l])
        h = modulate(rmsnorm(x, norm_ffn2[l]), sh3, sc3)
        x = x + 0.5 * g3[:, None, :] * swiglu(h, w_ffn2_in[l], w_ffn2_out[l])
    return rmsnorm(x, norm_final)
```

```python
import functools
import math

import jax
import jax.numpy as jnp
from jax import lax
from jax.experimental import pallas as pl
from jax.experimental.pallas import tpu as pltpu

F32 = jnp.float32
BF16 = jnp.bfloat16
HIGHEST = lax.Precision.HIGHEST

ATT_HEADS = 8
HEAD_DIM = 64
MOBA_BLOCK = 256
MOBA_TOPK = 3
ATT_SCALE = 1.0 / math.sqrt(HEAD_DIM)
SSM_GROUP = 16
SSM_STATE = 64
N_ADA = 9
EPS = 1e-6
NEG = -1e30

LANES = 128
HEAD_PAIR = LANES // HEAD_DIM
ALIBI_STEP = 8 // ATT_HEADS
assert ALIBI_STEP * ATT_HEADS == 8
VMEM_LIMIT = 56 * 1024 * 1024


def _params(sem, vmem=VMEM_LIMIT):
    return pltpu.CompilerParams(dimension_semantics=sem, vmem_limit_bytes=vmem)


def _norm_mod(x, g, shift, scale):
    y = x * lax.rsqrt(jnp.mean(x * x, axis=-1, keepdims=True) + EPS)
    return (y * g) * (1.0 + scale) + shift


def _ada_kernel(c_ref, w_ref, b_ref, o_ref):
    a = jax.nn.silu(c_ref[...])
    o_ref[...] = jnp.dot(a, w_ref[...], preferred_element_type=F32, precision=HIGHEST) + b_ref[...]


def ada_call(c, w, b):
    bsz, d = c.shape
    n = w.shape[1]
    tn = 1024
    return pl.pallas_call(
        _ada_kernel,
        grid=(n // tn,),
        in_specs=[pl.BlockSpec((bsz, d), lambda j: (0, 0)),
                  pl.BlockSpec((d, tn), lambda j: (0, j)),
                  pl.BlockSpec((1, tn), lambda j: (0, j))],
        out_specs=pl.BlockSpec((bsz, tn), lambda j: (0, j)),
        out_shape=jax.ShapeDtypeStruct((bsz, n), F32),
        compiler_params=_params(("arbitrary",)),
        name="ada",
    )(c, w, b.reshape(1, n))


def _ffn_kernel(x_ref, ada_ref, g_ref, wg_ref, wu_ref, wo_ref, gf_ref, o_ref, h_sc, acc_sc,
                *, row0, final_norm):
    f = pl.program_id(1)

    @pl.when(f == 0)
    def _():
        h = _norm_mod(x_ref[...], g_ref[...], ada_ref[row0:row0 + 1, :], ada_ref[row0 + 1:row0 + 2, :])
        h_sc[...] = h.astype(BF16)
        acc_sc[...] = jnp.zeros_like(acc_sc)

    h = h_sc[...]
    g = jnp.dot(h, wg_ref[...], preferred_element_type=F32)
    u = jnp.dot(h, wu_ref[...], preferred_element_type=F32)
    a = (jax.nn.silu(g) * u).astype(BF16)
    acc_sc[...] += jnp.dot(a, wo_ref[...], preferred_element_type=F32)

    @pl.when(f == pl.num_programs(1) - 1)
    def _():
        y = x_ref[...] + (0.5 * ada_ref[row0 + 2:row0 + 3, :]) * acc_sc[...]
        if final_norm:
            y = (y * lax.rsqrt(jnp.mean(y * y, axis=-1, keepdims=True) + EPS)) * gf_ref[...]
        o_ref[...] = y


def ffn_call(x2, ada3, g, w_in, w_out, gf, *, seq, row0, final_norm, tm=512, nf=2):
    t, d = x2.shape
    dff = w_out.shape[0]
    fc = dff // nf
    kern = functools.partial(_ffn_kernel, row0=row0, final_norm=final_norm)
    return pl.pallas_call(
        kern,
        grid=(t // tm, nf),
        in_specs=[pl.BlockSpec((tm, d), lambda i, f: (i, 0)),
                  pl.BlockSpec((None, N_ADA, d), lambda i, f: ((i * tm) // seq, 0, 0)),
                  pl.BlockSpec((1, d), lambda i, f: (0, 0)),
                  pl.BlockSpec((d, fc), lambda i, f: (0, f)),
                  pl.BlockSpec((d, fc), lambda i, f: (0, nf + f)),
                  pl.BlockSpec((fc, d), lambda i, f: (f, 0)),
                  pl.BlockSpec((1, d), lambda i, f: (0, 0))],
        out_specs=pl.BlockSpec((tm, d), lambda i, f: (i, 0)),
        out_shape=jax.ShapeDtypeStruct((t, d), F32),
        scratch_shapes=[pltpu.VMEM((tm, d), BF16), pltpu.VMEM((tm, d), F32)],
        compiler_params=_params(("arbitrary", "arbitrary")),
        name="ffn_final" if final_norm else "ffn",
    )(x2, ada3, g.reshape(1, d), w_in, w_in, w_out, gf.reshape(1, d))


def _inproj_kernel(x_ref, ada_ref, g_ref, w_ref, o_ref, h_sc):
    @pl.when(pl.program_id(1) == 0)
    def _():
        h = _norm_mod(x_ref[...], g_ref[...], ada_ref[3:4, :], ada_ref[4:5, :])
        h_sc[...] = h.astype(BF16)

    o_ref[...] = jnp.dot(h_sc[...], w_ref[...], preferred_element_type=F32)


def inproj_call(x2, ada3, g, w, *, seq, tm=1024, tn=1024):
    t, d = x2.shape
    n = w.shape[1]
    return pl.pallas_call(
        _inproj_kernel,
        grid=(t // tm, n // tn),
        in_specs=[pl.BlockSpec((tm, d), lambda i, j: (i, 0)),
                  pl.BlockSpec((None, N_ADA, d), lambda i, j: ((i * tm) // seq, 0, 0)),
                  pl.BlockSpec((1, d), lambda i, j: (0, 0)),
                  pl.BlockSpec((d, tn), lambda i, j: (0, j))],
        out_specs=pl.BlockSpec((tm, tn), lambda i, j: (i, j)),
        out_shape=jax.ShapeDtypeStruct((t, n), F32),
        scratch_shapes=[pltpu.VMEM((tm, d), BF16)],
        compiler_params=_params(("arbitrary", "arbitrary")),
        name="inproj",
    )(x2, ada3, g.reshape(1, d), w)


def _attn_kernel(q_ref, k_ref, v_ref, o_ref, kb_sc, vt_sc, km_sc, bias_sc, slope_sc, qb_sc, pen_sc,
                 m_sc, l_sc, acc_sc, *, nb):
    blk = MOBA_BLOCK
    p = pl.program_id(1)
    n = pl.program_id(2)

    @pl.when(n == 0)
    def _prep():
        def per_block(m, carry):
            r0 = pl.multiple_of(m * blk, blk)
            kblk = k_ref[pl.ds(r0, blk), :]
            kb_sc[m] = kblk.astype(BF16)
            km_sc[pl.ds(m, 1), :] = jnp.mean(kblk, axis=0, keepdims=True)
            vt_sc[m] = v_ref[pl.ds(r0, blk), :].T.astype(BF16)
            return carry

        lax.fori_loop(0, nb, per_block, 0)
        kj = lax.broadcasted_iota(jnp.int32, (blk, blk), 0)
        qi = lax.broadcasted_iota(jnp.int32, (blk, blk), 1)
        dm = (kj - qi).astype(F32)
        for hh in range(HEAD_PAIR):
            head = jnp.zeros((1, blk), jnp.int32) + (p * HEAD_PAIR + hh)
            bits = lax.shift_left(127 - ALIBI_STEP * (head + 1), 23)
            slope = lax.bitcast_convert_type(bits, F32)
            slope_sc[hh] = slope
            bias_sc[hh] = slope * dm

    q_t = q_ref[...].T
    drow = lax.broadcasted_iota(jnp.int32, (LANES, blk), 0)
    bidx = lax.broadcasted_iota(jnp.int32, (nb, blk), 0)
    kj = lax.broadcasted_iota(jnp.int32, (blk, blk), 0)
    qi = lax.broadcasted_iota(jnp.int32, (blk, blk), 1)
    causal = kj <= qi

    for hh in range(HEAD_PAIR):
        in_head = (drow >= hh * HEAD_DIM) & (drow < (hh + 1) * HEAD_DIM)
        q_h = jnp.where(in_head, q_t, 0.0)
        gate = jnp.dot(km_sc[...], q_h, preferred_element_type=F32, precision=HIGHEST)
        g = jnp.where(bidx < n, gate, NEG)
        sel = jnp.zeros((nb, blk), dtype=jnp.bool_)
        for j in range(min(MOBA_TOPK, nb)):
            mx = jnp.max(g, axis=0, keepdims=True)
            first = jnp.min(jnp.where(g == mx, bidx, nb), axis=0, keepdims=True)
            hit = bidx == first
            slot_valid = (jnp.zeros((1, blk), jnp.int32) + j) < n
            sel = sel | (hit & slot_valid)
            g = jnp.where(hit, -jnp.inf, g)
        pen_sc[hh] = jnp.where(sel, 0.0, NEG)
        qb = (q_h * ATT_SCALE).astype(BF16)
        qb_sc[hh] = qb
        s = jnp.dot(kb_sc[n], qb, preferred_element_type=F32) + bias_sc[hh]
        s = jnp.where(causal, s, NEG)
        m0 = jnp.max(s, axis=0, keepdims=True)
        e = jnp.exp(s - m0)
        m_sc[hh] = m0
        l_sc[hh] = jnp.sum(e, axis=0, keepdims=True)
        acc_sc[hh] = jnp.dot(vt_sc[n], e.astype(BF16), preferred_element_type=F32)

    def past_block(m, carry):
        kblk = kb_sc[m]
        vt = vt_sc[m]
        off = (n - m).astype(F32) * float(blk)
        for hh in range(HEAD_PAIR):
            row = pen_sc[hh, pl.ds(m, 1), :] - slope_sc[hh] * off
            s = jnp.dot(kblk, qb_sc[hh], preferred_element_type=F32) + bias_sc[hh] + row
            m_old = m_sc[hh]
            m_new = jnp.maximum(m_old, jnp.max(s, axis=0, keepdims=True))
            alpha = jnp.exp(m_old - m_new)
            e = jnp.exp(s - m_new)
            m_sc[hh] = m_new
            l_sc[hh] = alpha * l_sc[hh] + jnp.sum(e, axis=0, keepdims=True)
            acc_sc[hh] = alpha * acc_sc[hh] + jnp.dot(vt, e.astype(BF16), preferred_element_type=F32)
        return carry

    lax.fori_loop(0, n, past_block, 0)

    out_t = jnp.zeros((LANES, blk), F32)
    for hh in range(HEAD_PAIR):
        in_head = (drow >= hh * HEAD_DIM) & (drow < (hh + 1) * HEAD_DIM)
        out_t = jnp.where(in_head, acc_sc[hh] / l_sc[hh], out_t)
    o_ref[...] = out_t.T


def attn_call(y2, *, bsz, seq):
    blk = MOBA_BLOCK
    nb = seq // blk
    width = ATT_HEADS * HEAD_DIM
    pairs = width // LANES
    kern = functools.partial(_attn_kernel, nb=nb)
    return pl.pallas_call(
        kern,
        grid=(bsz, pairs, nb),
        in_specs=[pl.BlockSpec((blk, LANES), lambda b, p, n: (b * nb + n, p)),
                  pl.BlockSpec((seq, LANES), lambda b, p, n: (b, pairs + p)),
                  pl.BlockSpec((seq, LANES), lambda b, p, n: (b, 2 * pairs + p))],
        out_specs=pl.BlockSpec((blk, LANES), lambda b, p, n: (b * nb + n, p)),
        out_shape=jax.ShapeDtypeStruct((bsz * seq, width), F32),
        scratch_shapes=[pltpu.VMEM((nb, blk, LANES), BF16),
                        pltpu.VMEM((nb, LANES, blk), BF16),
                        pltpu.VMEM((nb, LANES), F32),
                        pltpu.VMEM((HEAD_PAIR, blk, blk), F32),
                        pltpu.VMEM((HEAD_PAIR, 1, blk), F32),
                        pltpu.VMEM((HEAD_PAIR, LANES, blk), BF16),
                        pltpu.VMEM((HEAD_PAIR, nb, blk), F32),
                        pltpu.VMEM((HEAD_PAIR, 1, blk), F32),
                        pltpu.VMEM((HEAD_PAIR, 1, blk), F32),
                        pltpu.VMEM((HEAD_PAIR, LANES, blk), F32)],
        compiler_params=_params(("arbitrary", "arbitrary", "arbitrary")),
        name="moba_attn",
    )(y2, y2, y2)


def _s5_param_kernel(lr_ref, li_ref, ldt_ref, bre_ref, bim_ref, w_ref, ar_ref, ai_ref, *, rows_per_half):
    lr = lr_ref[...]
    li = li_ref[...]
    dt = jnp.exp(ldt_ref[...])
    mag = jnp.exp(lr * dt)
    ang = li * dt
    ab_re = mag * jnp.cos(ang)
    ab_im = mag * jnp.sin(ang)
    nr = ab_re - 1.0
    ni = ab_im
    den = lr * lr + li * li
    f_re = (nr * lr + ni * li) / den
    f_im = (ni * lr - nr * li) / den
    half = lr.shape[1]
    kc = bre_ref.shape[0] // 2
    for gh in range(2):
        fr = f_re[gh:gh + 1, :]
        fi = f_im[gh:gh + 1, :]
        bre = bre_ref[gh * kc:(gh + 1) * kc, :]
        bim = bim_ref[gh * kc:(gh + 1) * kc, :]
        w_ref[gh * kc:(gh + 1) * kc, 0:half] = (fr * bre - fi * bim).astype(BF16)
        w_ref[gh * kc:(gh + 1) * kc, half:2 * half] = (fr * bim + fi * bre).astype(BF16)
        ar_ref[gh * rows_per_half:(gh + 1) * rows_per_half, :] = jnp.broadcast_to(
            ab_re[gh:gh + 1, :], (rows_per_half, half))
        ai_ref[gh * rows_per_half:(gh + 1) * rows_per_half, :] = jnp.broadcast_to(
            ab_im[gh:gh + 1, :], (rows_per_half, half))


def s5_param_call(lr2, li2, ldt2, bre_st, bim_st, *, bsz):
    half = lr2.shape[1]
    width = bre_st.shape[0]
    kern = functools.partial(_s5_param_kernel, rows_per_half=bsz)
    return pl.pallas_call(
        kern,
        out_shape=(jax.ShapeDtypeStruct((width, 2 * half), BF16),
                   jax.ShapeDtypeStruct((2 * bsz, half), F32),
                   jax.ShapeDtypeStruct((2 * bsz, half), F32)),
        compiler_params=pltpu.CompilerParams(vmem_limit_bytes=VMEM_LIMIT),
        name="s5_params",
    )(lr2, li2, ldt2, bre_st, bim_st)


def _s5_kernel(u_ref, w_ref, ar_ref, ai_ref, cre_ref, cim_ref, d_ref, wg_ref, bg_ref, o_ref,
               xs_sc, st_sc, *, bsz, tt):
    rows = 2 * bsz
    n_r = rows * tt
    width = u_ref.shape[2]
    kc = width // 2
    half = ar_ref.shape[1]
    t = pl.program_id(0)

    @pl.when(t == 0)
    def _():
        st_sc[...] = jnp.zeros_like(st_sc)

    lane = lax.broadcasted_iota(jnp.int32, (tt, width), 1)
    parts = []
    for gh in range(2):
        keep = (lane >= gh * kc) & (lane < (gh + 1) * kc)
        for b in range(bsz):
            parts.append(jnp.where(keep, u_ref[b], 0.0).astype(BF16))
    uz = jnp.concatenate(parts, axis=0)
    lt = tt.bit_length() - 1
    lr_ = rows.bit_length() - 1
    ro = lax.broadcasted_iota(jnp.int32, (n_r, n_r), 0)
    co = lax.broadcasted_iota(jnp.int32, (n_r, n_r), 1)
    src = lax.shift_left(jnp.bitwise_and(ro, rows - 1), lt) + lax.shift_right_logical(ro, lr_)
    perm = jnp.where(co == src, 1.0, 0.0).astype(BF16)
    up = jnp.dot(perm, uz, preferred_element_type=F32).astype(BF16)
    xs_sc[...] = jnp.dot(up, w_ref[...], preferred_element_type=F32)

    ar = ar_ref[...]
    ai = ai_ref[...]

    def step(i, carry):
        xr, xi = carry
        r0 = pl.multiple_of(i * rows, rows)
        br = xs_sc[pl.ds(r0, rows), 0:half]
        bi = xs_sc[pl.ds(r0, rows), half:2 * half]
        nr = ar * xr - ai * xi + br
        ni = ar * xi + ai * xr + bi
        xs_sc[pl.ds(r0, rows), 0:half] = nr
        xs_sc[pl.ds(r0, rows), half:2 * half] = ni
        return nr, ni

    xr, xi = lax.fori_loop(0, tt, step, (st_sc[:, 0:half], st_sc[:, half:2 * half]))
    st_sc[:, 0:half] = xr
    st_sc[:, half:2 * half] = xi

    y_all = (jnp.dot(xs_sc[:, 0:half].astype(BF16), cre_ref[...], preferred_element_type=F32)
             - jnp.dot(xs_sc[:, half:2 * half].astype(BF16), cim_ref[...], preferred_element_type=F32))
    rr = lax.broadcasted_iota(jnp.int32, (n_r, width), 0)
    ll = lax.broadcasted_iota(jnp.int32, (n_r, width), 1)
    gh_row = lax.shift_right_logical(jnp.bitwise_and(rr, rows - 1), lr_ - 1)
    valid = lax.shift_right_logical(ll, kc.bit_length() - 1) == gh_row
    yz = jnp.where(valid, y_all, 0.0).astype(BF16)
    ro2 = lax.broadcasted_iota(jnp.int32, (bsz * tt, n_r), 0)
    co2 = lax.broadcasted_iota(jnp.int32, (bsz * tt, n_r), 1)
    same_t = lax.shift_right_logical(co2, lr_) == jnp.bitwise_and(ro2, tt - 1)
    same_b = jnp.bitwise_and(co2, bsz - 1) == lax.shift_right_logical(ro2, lt)
    unperm = jnp.where(same_t & same_b, 1.0, 0.0).astype(BF16)
    ypre = jnp.dot(unperm, yz, preferred_element_type=F32)
    for b in range(bsz):
        y = ypre[b * tt:(b + 1) * tt, :] + d_ref[...] * u_ref[b]
        y = jax.nn.gelu(y)
        z = jnp.dot(y.astype(BF16), wg_ref[...], preferred_element_type=F32) + bg_ref[...]
        o_ref[b] = y * jax.nn.sigmoid(z)


def s5_call(y3, w_bu, ar, ai, cre_st, cim_st, d, wg, bg, *, tt=64):
    bsz, seq, _ = y3.shape
    width = d.shape[0]
    half = ar.shape[1]
    assert bsz & (bsz - 1) == 0 and tt & (tt - 1) == 0
    kern = functools.partial(_s5_kernel, bsz=bsz, tt=tt)
    const = lambda t: (0, 0)
    return pl.pallas_call(
        kern,
        grid=(seq // tt,),
        in_specs=[pl.BlockSpec((bsz, tt, width), lambda t: (0, t, 3)),
                  pl.BlockSpec(w_bu.shape, const),
                  pl.BlockSpec(ar.shape, const),
                  pl.BlockSpec(ai.shape, const),
                  pl.BlockSpec(cre_st.shape, const),
                  pl.BlockSpec(cim_st.shape, const),
                  pl.BlockSpec((1, width), const),
                  pl.BlockSpec(wg.shape, const),
                  pl.BlockSpec((1, width), const)],
        out_specs=pl.BlockSpec((bsz, tt, width), lambda t: (0, t, 0)),
        out_shape=jax.ShapeDtypeStruct((bsz, seq, width), F32),
        scratch_shapes=[pltpu.VMEM((2 * bsz * tt, 2 * half), F32),
                        pltpu.VMEM((2 * bsz, 2 * half), F32)],
        compiler_params=_params(("arbitrary",)),
        name="s5_scan",
    )(y3, w_bu, ar, ai, cre_st, cim_st, d.reshape(1, width), wg, bg.reshape(1, width))


def _merge_kernel(x_ref, ya_ref, ys_ref, ga_ref, gs_ref, ada_ref, wa_ref, ws_ref, wo_ref, o_ref):
    a = jnp.dot(ya_ref[...].astype(BF16), wa_ref[...], preferred_element_type=F32)
    s = jnp.dot(ys_ref[...].astype(BF16), ws_ref[...], preferred_element_type=F32)
    merged = jax.nn.sigmoid(ga_ref[...]) * a + jax.nn.sigmoid(gs_ref[...]) * s
    o_ref[...] = x_ref[...] + ada_ref[5:6, :] * jnp.dot(merged.astype(BF16), wo_ref[...],
                                                         preferred_element_type=F32)


def merge_call(x2, y_att, y_ssm, y2, ada3, wa, ws, wo, *, seq, tm=512):
    t, d = x2.shape
    width = y_att.shape[1]
    gcol = (y2.shape[1] - 2 * d) // d
    const = lambda i: (0, 0)
    return pl.pallas_call(
        _merge_kernel,
        grid=(t // tm,),
        in_specs=[pl.BlockSpec((tm, d), lambda i: (i, 0)),
                  pl.BlockSpec((tm, width), lambda i: (i, 0)),
                  pl.BlockSpec((tm, width), lambda i: (i, 0)),
                  pl.BlockSpec((tm, d), lambda i: (i, gcol)),
                  pl.BlockSpec((tm, d), lambda i: (i, gcol + 1)),
                  pl.BlockSpec((None, N_ADA, d), lambda i: ((i * tm) // seq, 0, 0)),
                  pl.BlockSpec(wa.shape, const),
                  pl.BlockSpec(ws.shape, const),
                  pl.BlockSpec(wo.shape, const)],
        out_specs=pl.BlockSpec((tm, d), lambda i: (i, 0)),
        out_shape=jax.ShapeDtypeStruct((t, d), F32),
        compiler_params=_params(("arbitrary",)),
        name="merge",
    )(x2, y_att, y_ssm, y2, y2, ada3, wa, ws, wo)


def _stack_block_diag(a):
    g, r, c = a.shape
    gl = g // 2
    eye = jnp.eye(gl, dtype=a.dtype)
    bd = a.reshape(2, gl, r, c)[:, :, :, None, :] * eye[None, :, None, :, None]
    return bd.reshape(2, gl * r, gl * c)


def s5_mixer(y3, lam_re, lam_im, log_dt, b_re, b_im, c_re, c_im, d_skip, w_glu, b_glu):
    bsz = y3.shape[0]
    g, p = lam_re.shape
    half = (g // 2) * p
    lr2 = lam_re.reshape(2, half)
    li2 = lam_im.reshape(2, half)
    ldt2 = jnp.broadcast_to(log_dt[:, None], (g, p)).reshape(2, half)
    bre_st = _stack_block_diag(jnp.swapaxes(b_re, 1, 2)).reshape(-1, half)
    bim_st = _stack_block_diag(jnp.swapaxes(b_im, 1, 2)).reshape(-1, half)
    w_bu, ar, ai = s5_param_call(lr2, li2, ldt2, bre_st, bim_st, bsz=bsz)
    cre_st = jnp.concatenate(list(_stack_block_diag(jnp.swapaxes(c_re, 1, 2))), axis=1).astype(BF16)
    cim_st = jnp.concatenate(list(_stack_block_diag(jnp.swapaxes(c_im, 1, 2))), axis=1).astype(BF16)
    return s5_call(y3, w_bu, ar, ai, cre_st, cim_st, d_skip, w_glu.astype(BF16), b_glu)


def kernel(x, c, w_ada, b_ada, norm_ffn1, w_ffn1_in, w_ffn1_out, norm_mix, w_in, lam_re, lam_im,
           log_dt, ssm_b_re, ssm_b_im, ssm_c_re, ssm_c_im, ssm_d, w_glu, b_glu, w_br_att, w_br_ssm,
           w_out, norm_ffn2, w_ffn2_in, w_ffn2_out, norm_final):
    bsz, seq, d = x.shape
    depth = w_ada.shape[0]
    x2 = x.reshape(bsz * seq, d)
    for l in range(depth):
        last = l == depth - 1
        ada3 = ada_call(c, w_ada[l], b_ada[l]).reshape(bsz, N_ADA, d)
        x2 = ffn_call(x2, ada3, norm_ffn1[l], w_ffn1_in[l].astype(BF16), w_ffn1_out[l].astype(BF16),
                      norm_final, seq=seq, row0=0, final_norm=False)
        y2 = inproj_call(x2, ada3, norm_mix[l], w_in[l].astype(BF16), seq=seq)
        y_att = attn_call(y2, bsz=bsz, seq=seq)
        y_ssm = s5_mixer(y2.reshape(bsz, seq, -1), lam_re[l], lam_im[l], log_dt[l], ssm_b_re[l],
                         ssm_b_im[l], ssm_c_re[l], ssm_c_im[l], ssm_d[l], w_glu[l], b_glu[l])
        x2 = merge_call(x2, y_att, y_ssm.reshape(bsz * seq, -1), y2, ada3, w_br_att[l].astype(BF16),
                        w_br_ssm[l].astype(BF16), w_out[l].astype(BF16), seq=seq)
        x2 = ffn_call(x2, ada3, norm_ffn2[l], w_ffn2_in[l].astype(BF16), w_ffn2_out[l].astype(BF16),
                      norm_final, seq=seq, row0=6, final_norm=last)
    if depth == 0:
        raise ValueError("depth must be positive")
    return x2.reshape(bsz, seq, d)
```

```python
import functools
import math

import jax
import jax.numpy as jnp
from jax import lax
from jax.experimental import pallas as pl
from jax.experimental.pallas import tpu as pltpu

F32 = jnp.float32
BF16 = jnp.bfloat16
HIGHEST = lax.Precision.HIGHEST

ATT_HEADS = 8
HEAD_DIM = 64
MOBA_BLOCK = 256
MOBA_TOPK = 3
ATT_SCALE = 1.0 / math.sqrt(HEAD_DIM)
SSM_GROUP = 16
SSM_STATE = 64
N_ADA = 9
EPS = 1e-6
NEG = -1e30
LOG2E = 1.0 / math.log(2.0)
ATT_KEY_GROUP = 4

LANES = 128
HEAD_PAIR = LANES // HEAD_DIM
ALIBI_STEP = 8 // ATT_HEADS
assert ALIBI_STEP * ATT_HEADS == 8
VMEM_LIMIT = 56 * 1024 * 1024


def _params(sem, vmem=VMEM_LIMIT):
    return pltpu.CompilerParams(dimension_semantics=sem, vmem_limit_bytes=vmem)


def _norm_mod(x, g, shift, scale):
    y = x * lax.rsqrt(jnp.mean(x * x, axis=-1, keepdims=True) + EPS)
    return (y * g) * (1.0 + scale) + shift


def _ada_kernel(c_ref, w_ref, b_ref, o_ref):
    a = jax.nn.silu(c_ref[...])
    o_ref[...] = jnp.dot(a, w_ref[...], preferred_element_type=F32, precision=HIGHEST) + b_ref[...]


def ada_call(c, w, b):
    bsz, d = c.shape
    n = w.shape[1]
    tn = 1024
    return pl.pallas_call(
        _ada_kernel,
        grid=(n // tn,),
        in_specs=[pl.BlockSpec((bsz, d), lambda j: (0, 0)),
                  pl.BlockSpec((d, tn), lambda j: (0, j)),
                  pl.BlockSpec((1, tn), lambda j: (0, j))],
        out_specs=pl.BlockSpec((bsz, tn), lambda j: (0, j)),
        out_shape=jax.ShapeDtypeStruct((bsz, n), F32),
        compiler_params=_params(("arbitrary",)),
        name="ada",
    )(c, w, b.reshape(1, n))


def _ffn_kernel(x_ref, ada_ref, g_ref, wg_ref, wu_ref, wo_ref, gf_ref, o_ref, h_sc, acc_sc,
                *, row0, final_norm):
    f = pl.program_id(1)

    @pl.when(f == 0)
    def _():
        h = _norm_mod(x_ref[...], g_ref[...], ada_ref[row0:row0 + 1, :], ada_ref[row0 + 1:row0 + 2, :])
        h_sc[...] = h.astype(BF16)
        acc_sc[...] = jnp.zeros_like(acc_sc)

    h = h_sc[...]
    g = jnp.dot(h, wg_ref[...], preferred_element_type=F32)
    u = jnp.dot(h, wu_ref[...], preferred_element_type=F32)
    a = (jax.nn.silu(g) * u).astype(BF16)
    acc_sc[...] += jnp.dot(a, wo_ref[...], preferred_element_type=F32)

    @pl.when(f == pl.num_programs(1) - 1)
    def _():
        y = x_ref[...] + (0.5 * ada_ref[row0 + 2:row0 + 3, :]) * acc_sc[...]
        if final_norm:
            y = (y * lax.rsqrt(jnp.mean(y * y, axis=-1, keepdims=True) + EPS)) * gf_ref[...]
        o_ref[...] = y


def ffn_call(x2, ada3, g, w_in, w_out, gf, *, seq, row0, final_norm, tm=512, nf=2):
    t, d = x2.shape
    dff = w_out.shape[0]
    fc = dff // nf
    kern = functools.partial(_ffn_kernel, row0=row0, final_norm=final_norm)
    return pl.pallas_call(
        kern,
        grid=(t // tm, nf),
        in_specs=[pl.BlockSpec((tm, d), lambda i, f: (i, 0)),
                  pl.BlockSpec((None, N_ADA, d), lambda i, f: ((i * tm) // seq, 0, 0)),
                  pl.BlockSpec((1, d), lambda i, f: (0, 0)),
                  pl.BlockSpec((d, fc), lambda i, f: (0, f)),
                  pl.BlockSpec((d, fc), lambda i, f: (0, nf + f)),
                  pl.BlockSpec((fc, d), lambda i, f: (f, 0)),
                  pl.BlockSpec((1, d), lambda i, f: (0, 0))],
        out_specs=pl.BlockSpec((tm, d), lambda i, f: (i, 0)),
        out_shape=jax.ShapeDtypeStruct((t, d), F32),
        scratch_shapes=[pltpu.VMEM((tm, d), BF16), pltpu.VMEM((tm, d), F32)],
        compiler_params=_params(("arbitrary", "arbitrary")),
        name="ffn_final" if final_norm else "ffn",
    )(x2, ada3, g.reshape(1, d), w_in, w_in, w_out, gf.reshape(1, d))


def _inproj_kernel(x_ref, ada_ref, g_ref, w_ref, o_ref, h_sc):
    @pl.when(pl.program_id(1) == 0)
    def _():
        h = _norm_mod(x_ref[...], g_ref[...], ada_ref[3:4, :], ada_ref[4:5, :])
        h_sc[...] = h.astype(BF16)

    o_ref[...] = jnp.dot(h_sc[...], w_ref[...], preferred_element_type=F32)


def inproj_call(x2, ada3, g, w, *, seq, tm=1024, tn=1024):
    t, d = x2.shape
    n = w.shape[1]
    return pl.pallas_call(
        _inproj_kernel,
        grid=(t // tm, n // tn),
        in_specs=[pl.BlockSpec((tm, d), lambda i, j: (i, 0)),
                  pl.BlockSpec((None, N_ADA, d), lambda i, j: ((i * tm) // seq, 0, 0)),
                  pl.BlockSpec((1, d), lambda i, j: (0, 0)),
                  pl.BlockSpec((d, tn), lambda i, j: (0, j))],
        out_specs=pl.BlockSpec((tm, tn), lambda i, j: (i, j)),
        out_shape=jax.ShapeDtypeStruct((t, n), F32),
        scratch_shapes=[pltpu.VMEM((tm, d), BF16)],
        compiler_params=_params(("arbitrary", "arbitrary")),
        name="inproj",
    )(x2, ada3, g.reshape(1, d), w)


def _attn_kernel(q_ref, k_ref, v_ref, o_ref, kb_sc, vt_sc, km_sc, bias_sc, slope_sc, qb_sc, pen_sc,
                 m_sc, l_sc, acc_sc, s_sc, *, nb, group):
    blk = MOBA_BLOCK
    p = pl.program_id(1)
    n = pl.program_id(2)

    @pl.when(n == 0)
    def _prep():
        def per_block(m, carry):
            r0 = pl.multiple_of(m * blk, blk)
            kblk = k_ref[pl.ds(r0, blk), :]
            kb_sc[pl.ds(r0, blk), :] = kblk.astype(BF16)
            km_sc[pl.ds(m, 1), :] = jnp.mean(kblk, axis=0, keepdims=True)
            vt_sc[m] = v_ref[pl.ds(r0, blk), :].T.astype(BF16)
            return carry

        lax.fori_loop(0, nb, per_block, 0)
        kj = lax.broadcasted_iota(jnp.int32, (blk, blk), 0)
        qi = lax.broadcasted_iota(jnp.int32, (blk, blk), 1)
        dm = (kj - qi).astype(F32)
        for hh in range(HEAD_PAIR):
            head = jnp.zeros((1, blk), jnp.int32) + (p * HEAD_PAIR + hh)
            bits = lax.shift_left(127 - ALIBI_STEP * (head + 1), 23)
            slope2 = lax.bitcast_convert_type(bits, F32) * LOG2E
            slope_sc[hh] = slope2
            bias_sc[hh] = slope2 * dm

    q_t = q_ref[...].T
    drow = lax.broadcasted_iota(jnp.int32, (LANES, blk), 0)
    bidx = lax.broadcasted_iota(jnp.int32, (nb, blk), 0)
    kj = lax.broadcasted_iota(jnp.int32, (blk, blk), 0)
    qi = lax.broadcasted_iota(jnp.int32, (blk, blk), 1)
    causal = kj <= qi
    own0 = pl.multiple_of(n * blk, blk)

    for hh in range(HEAD_PAIR):
        in_head = (drow >= hh * HEAD_DIM) & (drow < (hh + 1) * HEAD_DIM)
        q_h = jnp.where(in_head, q_t, 0.0)
        gate = jnp.dot(km_sc[...], q_h, preferred_element_type=F32, precision=HIGHEST)
        g = jnp.where(bidx < n, gate, NEG)
        sel = jnp.zeros((nb, blk), dtype=jnp.bool_)
        for j in range(min(MOBA_TOPK, nb)):
            mx = jnp.max(g, axis=0, keepdims=True)
            first = jnp.min(jnp.where(g == mx, bidx, nb), axis=0, keepdims=True)
            hit = bidx == first
            slot_valid = (jnp.zeros((1, blk), jnp.int32) + j) < n
            sel = sel | (hit & slot_valid)
            g = jnp.where(hit, -jnp.inf, g)
        pen_sc[hh] = jnp.where(sel, 0.0, NEG)
        qb = (q_h * (ATT_SCALE * LOG2E)).astype(BF16)
        qb_sc[hh] = qb
        s = jnp.dot(kb_sc[pl.ds(own0, blk), :], qb, preferred_element_type=F32) + bias_sc[hh]
        s = jnp.where(causal, s, NEG)
        m0 = jnp.max(s, axis=0, keepdims=True)
        e = jnp.exp2(s - m0)
        m_sc[hh] = m0
        l_sc[hh] = jnp.sum(e, axis=0, keepdims=True)
        acc_sc[hh] = jnp.dot(vt_sc[n], e.astype(BF16), preferred_element_type=F32)

    def past_group(gi, carry):
        base = gi * group
        k0 = pl.multiple_of(base * blk, group * blk)
        kg = kb_sc[pl.ds(k0, group * blk), :]
        for hh in range(HEAD_PAIR):
            s_sc[hh] = jnp.dot(kg, qb_sc[hh], preferred_element_type=F32)
        for hh in range(HEAD_PAIR):
            m_old = m_sc[hh]
            m_new = m_old
            for k in range(group):
                off = (n - (base + k)).astype(F32) * float(blk)
                row = pen_sc[hh, pl.ds(base + k, 1), :] - slope_sc[hh] * off
                s = s_sc[hh, k * blk:(k + 1) * blk, :] + bias_sc[hh] + row
                s_sc[hh, k * blk:(k + 1) * blk, :] = s
                m_new = jnp.maximum(m_new, jnp.max(s, axis=0, keepdims=True))
            alpha = jnp.exp2(m_old - m_new)
            l_new = alpha * l_sc[hh]
            acc = alpha * acc_sc[hh]
            for k in range(group):
                e = jnp.exp2(s_sc[hh, k * blk:(k + 1) * blk, :] - m_new)
                l_new = l_new + jnp.sum(e, axis=0, keepdims=True)
                acc = acc + jnp.dot(vt_sc[base + k], e.astype(BF16), preferred_element_type=F32)
            m_sc[hh] = m_new
            l_sc[hh] = l_new
            acc_sc[hh] = acc
        return carry

    lax.fori_loop(0, lax.div(n + (group - 1), group), past_group, 0)

    out_t = jnp.zeros((LANES, blk), F32)
    for hh in range(HEAD_PAIR):
        in_head = (drow >= hh * HEAD_DIM) & (drow < (hh + 1) * HEAD_DIM)
        out_t = jnp.where(in_head, acc_sc[hh] / l_sc[hh], out_t)
    o_ref[...] = out_t.T


def attn_call(y2, *, bsz, seq):
    blk = MOBA_BLOCK
    nb = seq // blk
    width = ATT_HEADS * HEAD_DIM
    pairs = width // LANES
    group = math.gcd(nb, ATT_KEY_GROUP)
    kern = functools.partial(_attn_kernel, nb=nb, group=group)
    return pl.pallas_call(
        kern,
        grid=(bsz, pairs, nb),
        in_specs=[pl.BlockSpec((blk, LANES), lambda b, p, n: (b * nb + n, p)),
                  pl.BlockSpec((seq, LANES), lambda b, p, n: (b, pairs + p)),
                  pl.BlockSpec((seq, LANES), lambda b, p, n: (b, 2 * pairs + p))],
        out_specs=pl.BlockSpec((blk, LANES), lambda b, p, n: (b * nb + n, p)),
        out_shape=jax.ShapeDtypeStruct((bsz * seq, width), F32),
        scratch_shapes=[pltpu.VMEM((nb * blk, LANES), BF16),
                        pltpu.VMEM((nb, LANES, blk), BF16),
                        pltpu.VMEM((nb, LANES), F32),
                        pltpu.VMEM((HEAD_PAIR, blk, blk), F32),
                        pltpu.VMEM((HEAD_PAIR, 1, blk), F32),
                        pltpu.VMEM((HEAD_PAIR, LANES, blk), BF16),
                        pltpu.VMEM((HEAD_PAIR, nb, blk), F32),
                        pltpu.VMEM((HEAD_PAIR, 1, blk), F32),
                        pltpu.VMEM((HEAD_PAIR, 1, blk), F32),
                        pltpu.VMEM((HEAD_PAIR, LANES, blk), F32),
                        pltpu.VMEM((HEAD_PAIR, group * blk, blk), F32)],
        compiler_params=_params(("arbitrary", "arbitrary", "arbitrary")),
        name="moba_attn",
    )(y2, y2, y2)


def _s5_param_kernel(lr_ref, li_ref, ldt_ref, bre_ref, bim_ref, w_ref, ar_ref, ai_ref, *, rows_per_half):
    lr = lr_ref[...]
    li = li_ref[...]
    dt = jnp.exp(ldt_ref[...])
    mag = jnp.exp(lr * dt)
    ang = li * dt
    ab_re = mag * jnp.cos(ang)
    ab_im = mag * jnp.sin(ang)
    nr = ab_re - 1.0
    ni = ab_im
    den = lr * lr + li * li
    f_re = (nr * lr + ni * li) / den
    f_im = (ni * lr - nr * li) / den
    half = lr.shape[1]
    kc = bre_ref.shape[0] // 2
    for gh in range(2):
        fr = f_re[gh:gh + 1, :]
        fi = f_im[gh:gh + 1, :]
        bre = bre_ref[gh * kc:(gh + 1) * kc, :]
        bim = bim_ref[gh * kc:(gh + 1) * kc, :]
        w_ref[gh * kc:(gh + 1) * kc, 0:half] = (fr * bre - fi * bim).astype(BF16)
        w_ref[gh * kc:(gh + 1) * kc, half:2 * half] = (fr * bim + fi * bre).astype(BF16)
        ar_ref[gh * rows_per_half:(gh + 1) * rows_per_half, :] = jnp.broadcast_to(
            ab_re[gh:gh + 1, :], (rows_per_half, half))
        ai_ref[gh * rows_per_half:(gh + 1) * rows_per_half, :] = jnp.broadcast_to(
            ab_im[gh:gh + 1, :], (rows_per_half, half))


def s5_param_call(lr2, li2, ldt2, bre_st, bim_st, *, bsz):
    half = lr2.shape[1]
    width = bre_st.shape[0]
    kern = functools.partial(_s5_param_kernel, rows_per_half=bsz)
    return pl.pallas_call(
        kern,
        out_shape=(jax.ShapeDtypeStruct((width, 2 * half), BF16),
                   jax.ShapeDtypeStruct((2 * bsz, half), F32),
                   jax.ShapeDtypeStruct((2 * bsz, half), F32)),
        compiler_params=pltpu.CompilerParams(vmem_limit_bytes=VMEM_LIMIT),
        name="s5_params",
    )(lr2, li2, ldt2, bre_st, bim_st)


def _s5_kernel(u_ref, w_ref, ar_ref, ai_ref, cre_ref, cim_ref, d_ref, wg_ref, bg_ref, o_ref,
               xs_sc, st_sc, *, bsz, tt):
    rows = 2 * bsz
    n_r = rows * tt
    width = u_ref.shape[2]
    kc = width // 2
    half = ar_ref.shape[1]
    t = pl.program_id(0)

    @pl.when(t == 0)
    def _():
        st_sc[...] = jnp.zeros_like(st_sc)

    lane = lax.broadcasted_iota(jnp.int32, (tt, width), 1)
    parts = []
    for gh in range(2):
        keep = (lane >= gh * kc) & (lane < (gh + 1) * kc)
        for b in range(bsz):
            parts.append(jnp.where(keep, u_ref[b], 0.0).astype(BF16))
    uz = jnp.concatenate(parts, axis=0)
    lt = tt.bit_length() - 1
    lr_ = rows.bit_length() - 1
    ro = lax.broadcasted_iota(jnp.int32, (n_r, n_r), 0)
    co = lax.broadcasted_iota(jnp.int32, (n_r, n_r), 1)
    src = lax.shift_left(jnp.bitwise_and(ro, rows - 1), lt) + lax.shift_right_logical(ro, lr_)
    perm = jnp.where(co == src, 1.0, 0.0).astype(BF16)
    up = jnp.dot(perm, uz, preferred_element_type=F32).astype(BF16)
    xs_sc[...] = jnp.dot(up, w_ref[...], preferred_element_type=F32)

    ar = ar_ref[...]
    ai = ai_ref[...]

    def step(i, carry):
        xr, xi = carry
        r0 = pl.multiple_of(i * rows, rows)
        br = xs_sc[pl.ds(r0, rows), 0:half]
        bi = xs_sc[pl.ds(r0, rows), half:2 * half]
        nr = ar * xr - ai * xi + br
        ni = ar * xi + ai * xr + bi
        xs_sc[pl.ds(r0, rows), 0:half] = nr
        xs_sc[pl.ds(r0, rows), half:2 * half] = ni
        return nr, ni

    xr, xi = lax.fori_loop(0, tt, step, (st_sc[:, 0:half], st_sc[:, half:2 * half]))
    st_sc[:, 0:half] = xr
    st_sc[:, half:2 * half] = xi

    y_all = (jnp.dot(xs_sc[:, 0:half].astype(BF16), cre_ref[...], preferred_element_type=F32)
             - jnp.dot(xs_sc[:, half:2 * half].astype(BF16), cim_ref[...], preferred_element_type=F32))
    rr = lax.broadcasted_iota(jnp.int32, (n_r, width), 0)
    ll = lax.broadcasted_iota(jnp.int32, (n_r, width), 1)
    gh_row = lax.shift_right_logical(jnp.bitwise_and(rr, rows - 1), lr_ - 1)
    valid = lax.shift_right_logical(ll, kc.bit_length() - 1) == gh_row
    yz = jnp.where(valid, y_all, 0.0).astype(BF16)
    ro2 = lax.broadcasted_iota(jnp.int32, (bsz * tt, n_r), 0)
    co2 = lax.broadcasted_iota(jnp.int32, (bsz * tt, n_r), 1)
    same_t = lax.shift_right_logical(co2, lr_) == jnp.bitwise_and(ro2, tt - 1)
    same_b = jnp.bitwise_and(co2, bsz - 1) == lax.shift_right_logical(ro2, lt)
    unperm = jnp.where(same_t & same_b, 1.0, 0.0).astype(BF16)
    ypre = jnp.dot(unperm, yz, preferred_element_type=F32)
    for b in range(bsz):
        y = ypre[b * tt:(b + 1) * tt, :] + d_ref[...] * u_ref[b]
        y = jax.nn.gelu(y)
        z = jnp.dot(y.astype(BF16), wg_ref[...], preferred_element_type=F32) + bg_ref[...]
        o_ref[b] = y * jax.nn.sigmoid(z)


def s5_call(y3, w_bu, ar, ai, cre_st, cim_st, d, wg, bg, *, tt=64):
    bsz, seq, _ = y3.shape
    width = d.shape[0]
    half = ar.shape[1]
    assert bsz & (bsz - 1) == 0 and tt & (tt - 1) == 0
    kern = functools.partial(_s5_kernel, bsz=bsz, tt=tt)
    const = lambda t: (0, 0)
    return pl.pallas_call(
        kern,
        grid=(seq // tt,),
        in_specs=[pl.BlockSpec((bsz, tt, width), lambda t: (0, t, 3)),
                  pl.BlockSpec(w_bu.shape, const),
                  pl.BlockSpec(ar.shape, const),
                  pl.BlockSpec(ai.shape, const),
                  pl.BlockSpec(cre_st.shape, const),
                  pl.BlockSpec(cim_st.shape, const),
                  pl.BlockSpec((1, width), const),
                  pl.BlockSpec(wg.shape, const),
                  pl.BlockSpec((1, width), const)],
        out_specs=pl.BlockSpec((bsz, tt, width), lambda t: (0, t, 0)),
        out_shape=jax.ShapeDtypeStruct((bsz, seq, width), F32),
        scratch_shapes=[pltpu.VMEM((2 * bsz * tt, 2 * half), F32),
                        pltpu.VMEM((2 * bsz, 2 * half), F32)],
        compiler_params=_params(("arbitrary",)),
        name="s5_scan",
    )(y3, w_bu, ar, ai, cre_st, cim_st, d.reshape(1, width), wg, bg.reshape(1, width))


def _merge_kernel(x_ref, ya_ref, ys_ref, ga_ref, gs_ref, ada_ref, wa_ref, ws_ref, wo_ref, o_ref):
    a = jnp.dot(ya_ref[...].astype(BF16), wa_ref[...], preferred_element_type=F32)
    s = jnp.dot(ys_ref[...].astype(BF16), ws_ref[...], preferred_element_type=F32)
    merged = jax.nn.sigmoid(ga_ref[...]) * a + jax.nn.sigmoid(gs_ref[...]) * s
    o_ref[...] = x_ref[...] + ada_ref[5:6, :] * jnp.dot(merged.astype(BF16), wo_ref[...],
                                                         preferred_element_type=F32)


def merge_call(x2, y_att, y_ssm, y2, ada3, wa, ws, wo, *, seq, tm=512):
    t, d = x2.shape
    width = y_att.shape[1]
    gcol = (y2.shape[1] - 2 * d) // d
    const = lambda i: (0, 0)
    return pl.pallas_call(
        _merge_kernel,
        grid=(t // tm,),
        in_specs=[pl.BlockSpec((tm, d), lambda i: (i, 0)),
                  pl.BlockSpec((tm, width), lambda i: (i, 0)),
                  pl.BlockSpec((tm, width), lambda i: (i, 0)),
                  pl.BlockSpec((tm, d), lambda i: (i, gcol)),
                  pl.BlockSpec((tm, d), lambda i: (i, gcol + 1)),
                  pl.BlockSpec((None, N_ADA, d), lambda i: ((i * tm) // seq, 0, 0)),
                  pl.BlockSpec(wa.shape, const),
                  pl.BlockSpec(ws.shape, const),
                  pl.BlockSpec(wo.shape, const)],
        out_specs=pl.BlockSpec((tm, d), lambda i: (i, 0)),
        out_shape=jax.ShapeDtypeStruct((t, d), F32),
        compiler_params=_params(("arbitrary",)),
        name="merge",
    )(x2, y_att, y_ssm, y2, y2, ada3, wa, ws, wo)


def _stack_block_diag(a):
    g, r, c = a.shape
    gl = g // 2
    eye = jnp.eye(gl, dtype=a.dtype)
    bd = a.reshape(2, gl, r, c)[:, :, :, None, :] * eye[None, :, None, :, None]
    return bd.reshape(2, gl * r, gl * c)


def s5_mixer(y3, lam_re, lam_im, log_dt, b_re, b_im, c_re, c_im, d_skip, w_glu, b_glu):
    bsz = y3.shape[0]
    g, p = lam_re.shape
    half = (g // 2) * p
    lr2 = lam_re.reshape(2, half)
    li2 = lam_im.reshape(2, half)
    ldt2 = jnp.broadcast_to(log_dt[:, None], (g, p)).reshape(2, half)
    bre_st = _stack_block_diag(jnp.swapaxes(b_re, 1, 2)).reshape(-1, half)
    bim_st = _stack_block_diag(jnp.swapaxes(b_im, 1, 2)).reshape(-1, half)
    w_bu, ar, ai = s5_param_call(lr2, li2, ldt2, bre_st, bim_st, bsz=bsz)
    cre_st = jnp.concatenate(list(_stack_block_diag(jnp.swapaxes(c_re, 1, 2))), axis=1).astype(BF16)
    cim_st = jnp.concatenate(list(_stack_block_diag(jnp.swapaxes(c_im, 1, 2))), axis=1).astype(BF16)
    return s5_call(y3, w_bu, ar, ai, cre_st, cim_st, d_skip, w_glu.astype(BF16), b_glu)


def kernel(x, c, w_ada, b_ada, norm_ffn1, w_ffn1_in, w_ffn1_out, norm_mix, w_in, lam_re, lam_im,
           log_dt, ssm_b_re, ssm_b_im, ssm_c_re, ssm_c_im, ssm_d, w_glu, b_glu, w_br_att, w_br_ssm,
           w_out, norm_ffn2, w_ffn2_in, w_ffn2_out, norm_final):
    bsz, seq, d = x.shape
    depth = w_ada.shape[0]
    x2 = x.reshape(bsz * seq, d)
    for l in range(depth):
        last = l == depth - 1
        ada3 = ada_call(c, w_ada[l], b_ada[l]).reshape(bsz, N_ADA, d)
        x2 = ffn_call(x2, ada3, norm_ffn1[l], w_ffn1_in[l].astype(BF16), w_ffn1_out[l].astype(BF16),
                      norm_final, seq=seq, row0=0, final_norm=False)
        y2 = inproj_call(x2, ada3, norm_mix[l], w_in[l].astype(BF16), seq=seq)
        y_att = attn_call(y2, bsz=bsz, seq=seq)
        y_ssm = s5_mixer(y2.reshape(bsz, seq, -1), lam_re[l], lam_im[l], log_dt[l], ssm_b_re[l],
                         ssm_b_im[l], ssm_c_re[l], ssm_c_im[l], ssm_d[l], w_glu[l], b_glu[l])
        x2 = merge_call(x2, y_att, y_ssm.reshape(bsz * seq, -1), y2, ada3, w_br_att[l].astype(BF16),
                        w_br_ssm[l].astype(BF16), w_out[l].astype(BF16), seq=seq)
        x2 = ffn_call(x2, ada3, norm_ffn2[l], w_ffn2_in[l].astype(BF16), w_ffn2_out[l].astype(BF16),
                      norm_final, seq=seq, row0=6, final_norm=last)
    if depth == 0:
        raise ValueError("depth must be positive")
    return x2.reshape(bsz, seq, d)
```

```python
import functools
import math

import jax
import jax.numpy as jnp
from jax import lax
from jax.experimental import pallas as pl
from jax.experimental.pallas import tpu as pltpu

F32 = jnp.float32
BF16 = jnp.bfloat16
HIGHEST = lax.Precision.HIGHEST

ATT_HEADS = 8
HEAD_DIM = 64
MOBA_BLOCK = 256
MOBA_TOPK = 3
ATT_SCALE = 1.0 / math.sqrt(HEAD_DIM)
SSM_GROUP = 16
SSM_STATE = 64
N_ADA = 9
EPS = 1e-6
NEG = -1e30
LOG2E = 1.0 / math.log(2.0)
ATT_KEY_GROUP = 4
AUX_PEN0 = 8
AUX_PEN_SLOTS = 8
AUX_ROWS = AUX_PEN0 + AUX_PEN_SLOTS

LANES = 128
HEAD_PAIR = LANES // HEAD_DIM
ALIBI_STEP = 8 // ATT_HEADS
assert ALIBI_STEP * ATT_HEADS == 8
VMEM_LIMIT = 56 * 1024 * 1024


def _params(sem, vmem=VMEM_LIMIT):
    return pltpu.CompilerParams(dimension_semantics=sem, vmem_limit_bytes=vmem)


def _norm_mod(x, g, shift, scale):
    y = x * lax.rsqrt(jnp.mean(x * x, axis=-1, keepdims=True) + EPS)
    return (y * g) * (1.0 + scale) + shift


def _ada_kernel(c_ref, w_ref, b_ref, o_ref):
    a = jax.nn.silu(c_ref[...])
    o_ref[...] = jnp.dot(a, w_ref[...], preferred_element_type=F32, precision=HIGHEST) + b_ref[...]


def ada_call(c, w, b):
    bsz, d = c.shape
    n = w.shape[1]
    tn = 1024
    return pl.pallas_call(
        _ada_kernel,
        grid=(n // tn,),
        in_specs=[pl.BlockSpec((bsz, d), lambda j: (0, 0)),
                  pl.BlockSpec((d, tn), lambda j: (0, j)),
                  pl.BlockSpec((1, tn), lambda j: (0, j))],
        out_specs=pl.BlockSpec((bsz, tn), lambda j: (0, j)),
        out_shape=jax.ShapeDtypeStruct((bsz, n), F32),
        compiler_params=_params(("arbitrary",)),
        name="ada",
    )(c, w, b.reshape(1, n))


def _ffn_kernel(x_ref, ada_ref, g_ref, wg_ref, wu_ref, wo_ref, gf_ref, o_ref, h_sc, acc_sc,
                *, row0, final_norm):
    f = pl.program_id(1)

    @pl.when(f == 0)
    def _():
        h = _norm_mod(x_ref[...], g_ref[...], ada_ref[row0:row0 + 1, :], ada_ref[row0 + 1:row0 + 2, :])
        h_sc[...] = h.astype(BF16)
        acc_sc[...] = jnp.zeros_like(acc_sc)

    h = h_sc[...]
    g = jnp.dot(h, wg_ref[...], preferred_element_type=F32)
    u = jnp.dot(h, wu_ref[...], preferred_element_type=F32)
    a = (jax.nn.silu(g) * u).astype(BF16)
    acc_sc[...] += jnp.dot(a, wo_ref[...], preferred_element_type=F32)

    @pl.when(f == pl.num_programs(1) - 1)
    def _():
        y = x_ref[...] + (0.5 * ada_ref[row0 + 2:row0 + 3, :]) * acc_sc[...]
        if final_norm:
            y = (y * lax.rsqrt(jnp.mean(y * y, axis=-1, keepdims=True) + EPS)) * gf_ref[...]
        o_ref[...] = y


def ffn_call(x2, ada3, g, w_in, w_out, gf, *, seq, row0, final_norm, tm=512, nf=2):
    t, d = x2.shape
    dff = w_out.shape[0]
    fc = dff // nf
    kern = functools.partial(_ffn_kernel, row0=row0, final_norm=final_norm)
    return pl.pallas_call(
        kern,
        grid=(t // tm, nf),
        in_specs=[pl.BlockSpec((tm, d), lambda i, f: (i, 0)),
                  pl.BlockSpec((None, N_ADA, d), lambda i, f: ((i * tm) // seq, 0, 0)),
                  pl.BlockSpec((1, d), lambda i, f: (0, 0)),
                  pl.BlockSpec((d, fc), lambda i, f: (0, f)),
                  pl.BlockSpec((d, fc), lambda i, f: (0, nf + f)),
                  pl.BlockSpec((fc, d), lambda i, f: (f, 0)),
                  pl.BlockSpec((1, d), lambda i, f: (0, 0))],
        out_specs=pl.BlockSpec((tm, d), lambda i, f: (i, 0)),
        out_shape=jax.ShapeDtypeStruct((t, d), F32),
        scratch_shapes=[pltpu.VMEM((tm, d), BF16), pltpu.VMEM((tm, d), F32)],
        compiler_params=_params(("arbitrary", "arbitrary")),
        name="ffn_final" if final_norm else "ffn",
    )(x2, ada3, g.reshape(1, d), w_in, w_in, w_out, gf.reshape(1, d))


def _inproj_kernel(x_ref, ada_ref, g_ref, w_ref, o_ref, h_sc):
    @pl.when(pl.program_id(1) == 0)
    def _():
        h = _norm_mod(x_ref[...], g_ref[...], ada_ref[3:4, :], ada_ref[4:5, :])
        h_sc[...] = h.astype(BF16)

    o_ref[...] = jnp.dot(h_sc[...], w_ref[...], preferred_element_type=F32)


def inproj_call(x2, ada3, g, w, *, seq, tm=1024, tn=1024):
    t, d = x2.shape
    n = w.shape[1]
    return pl.pallas_call(
        _inproj_kernel,
        grid=(t // tm, n // tn),
        in_specs=[pl.BlockSpec((tm, d), lambda i, j: (i, 0)),
                  pl.BlockSpec((None, N_ADA, d), lambda i, j: ((i * tm) // seq, 0, 0)),
                  pl.BlockSpec((1, d), lambda i, j: (0, 0)),
                  pl.BlockSpec((d, tn), lambda i, j: (0, j))],
        out_specs=pl.BlockSpec((tm, tn), lambda i, j: (i, j)),
        out_shape=jax.ShapeDtypeStruct((t, n), F32),
        scratch_shapes=[pltpu.VMEM((tm, d), BF16)],
        compiler_params=_params(("arbitrary", "arbitrary")),
        name="inproj",
    )(x2, ada3, g.reshape(1, d), w)


def _split_bf16(x):
    hi = x.astype(BF16).astype(F32)
    return hi, x - hi


def _attn_kernel(q_ref, k_ref, v_ref, o_ref, kb_sc, vtg_sc, vto_sc, km_sc, kmm_sc, aux_sc, slope_sc, auxq_sc,
                 qb_sc, pen_sc, m_sc, acc_sc, sa_sc, sb_sc, ga_sc, gb_sc, e_sc, *, nb, group):
    blk = MOBA_BLOCK
    p = pl.program_id(1)
    n = pl.program_id(2)
    aux0 = [((hh + 1) % HEAD_PAIR) * HEAD_DIM for hh in range(HEAD_PAIR)]

    @pl.when(n == 0)
    def _prep():
        lane = lax.broadcasted_iota(jnp.int32, (blk, LANES), 1)
        jrow = lax.broadcasted_iota(jnp.int32, (blk, LANES), 0).astype(F32)
        drow_v = lax.broadcasted_iota(jnp.int32, (LANES, blk), 0)

        def per_group(gi, carry):
            for k in range(group):
                m = gi * group + k
                r0 = pl.multiple_of(m * blk, blk)
                kblk = k_ref[pl.ds(r0, blk), :]
                km_sc[pl.ds(m, 1), :] = jnp.mean(kblk, axis=0, keepdims=True)
                vt = v_ref[pl.ds(r0, blk), :].T
                mpos = (m * blk).astype(F32)
                res = jnp.bitwise_and(m, AUX_PEN_SLOTS - 1)
                for hh in range(HEAD_PAIR):
                    a = lane - aux0[hh]
                    feat = jnp.where((a == 0) | (a == 1), jrow,
                                     jnp.where((a == 2) | (a == 3), mpos,
                                               jnp.where((a == 4) | (a == 5) | ((a - AUX_PEN0) == res), 1.0, 0.0)))
                    feat = jnp.where((a >= 0) & (a < AUX_ROWS), feat, 0.0)
                    own = (lane >= hh * HEAD_DIM) & (lane < (hh + 1) * HEAD_DIM)
                    kb_sc[hh, pl.ds(r0, blk), :] = jnp.where(own, kblk, feat).astype(BF16)
                    own_r = (drow_v >= hh * HEAD_DIM) & (drow_v < (hh + 1) * HEAD_DIM)
                    vth = jnp.where(own_r, vt, jnp.where(drow_v == aux0[hh], 1.0, 0.0)).astype(BF16)
                    vto_sc[hh, m] = vth
                    vtg_sc[hh, gi, :, k * blk:(k + 1) * blk] = vth
            return carry

        lax.fori_loop(0, nb // group, per_group, 0)
        lane_m = lax.broadcasted_iota(jnp.int32, (nb, LANES), 1)
        for hh in range(HEAD_PAIR):
            own_m = (lane_m >= hh * HEAD_DIM) & (lane_m < (hh + 1) * HEAD_DIM)
            kmm_sc[hh * nb:(hh + 1) * nb, :] = jnp.where(own_m, km_sc[...], 0.0)
        r8 = lax.broadcasted_iota(jnp.int32, (AUX_PEN0, blk), 0)
        for hh in range(HEAD_PAIR):
            head = jnp.zeros((AUX_PEN0, blk), jnp.int32) + (p * HEAD_PAIR + hh)
            bits = lax.shift_left(127 - ALIBI_STEP * (head + 1), 23)
            slope2 = lax.bitcast_convert_type(bits, F32) * LOG2E
            hi, lo = _split_bf16(slope2)
            slope_sc[hh] = slope2
            aux_sc[hh] = jnp.where((r8 == 0) | (r8 == 2), hi, jnp.where((r8 == 1) | (r8 == 3), lo, 0.0))

    q_t = q_ref[...].T
    drow = lax.broadcasted_iota(jnp.int32, (LANES, blk), 0)
    bidx = lax.broadcasted_iota(jnp.int32, (nb, blk), 0)
    kj = lax.broadcasted_iota(jnp.int32, (blk, blk), 0)
    qi = lax.broadcasted_iota(jnp.int32, (blk, blk), 1)
    causal = kj <= qi
    own0 = pl.multiple_of(n * blk, blk)
    qpos = (n * blk).astype(F32)
    r8 = lax.broadcasted_iota(jnp.int32, (AUX_PEN0, blk), 0)

    s_own = []
    for hh in range(HEAD_PAIR):
        in_head = (drow >= hh * HEAD_DIM) & (drow < (hh + 1) * HEAD_DIM)
        q_h = jnp.where(in_head, q_t, 0.0)
        off_hi, off_lo = _split_bf16(-(slope_sc[hh] * qpos))
        auxq = jnp.where(r8 == 4, off_hi, jnp.where(r8 == 5, off_lo, aux_sc[hh]))
        auxq_sc[hh] = auxq
        qb_sc[hh] = (q_h * (ATT_SCALE * LOG2E)).astype(BF16)
        a0 = aux0[hh]
        qb_sc[hh, a0:a0 + AUX_ROWS, :] = jnp.concatenate(
            [auxq, jnp.zeros((AUX_ROWS - AUX_PEN0, blk), F32)], axis=0).astype(BF16)
        s_own.append(jnp.dot(kb_sc[hh, pl.ds(own0, blk), :], qb_sc[hh], preferred_element_type=F32))

    gate_all = jnp.dot(kmm_sc[...], q_t, preferred_element_type=F32, precision=HIGHEST)
    for hh in range(HEAD_PAIR):
        g = jnp.where(bidx < n, gate_all[hh * nb:(hh + 1) * nb, :], NEG)
        sel = jnp.zeros((nb, blk), dtype=jnp.bool_)
        for j in range(min(MOBA_TOPK, nb)):
            mx = jnp.max(g, axis=0, keepdims=True)
            first = jnp.min(jnp.where(g == mx, bidx, nb), axis=0, keepdims=True)
            hit = bidx == first
            slot_valid = (jnp.zeros((1, blk), jnp.int32) + j) < n
            sel = sel | (hit & slot_valid)
            g = jnp.where(hit, -jnp.inf, g)
        pen_sc[hh] = jnp.where(sel, 0.0, NEG)

    for hh in range(HEAD_PAIR):
        s = jnp.where(causal, s_own[hh], NEG)
        m0 = jnp.max(s, axis=0, keepdims=True)
        e = jnp.exp2(s - m0)
        m_sc[hh] = m0
        acc_sc[hh] = jnp.dot(vto_sc[hh, n], e.astype(BF16), preferred_element_type=F32)

    bufs = ((sa_sc, ga_sc), (sb_sc, gb_sc))

    def score_stage(g, buf):
        s_sc, gmax_sc = buf
        base = g * group
        k0 = pl.multiple_of(base * blk, group * blk)
        b8 = pl.multiple_of(lax.shift_left(lax.shift_right_logical(base, 3), 3), AUX_PEN_SLOTS)
        for hh in range(HEAD_PAIR):
            a0 = aux0[hh]
            pen8 = pen_sc[hh, pl.ds(b8, AUX_PEN_SLOTS), :]
            qb_sc[hh, a0:a0 + AUX_ROWS, :] = jnp.concatenate([auxq_sc[hh], pen8], axis=0).astype(BF16)
        for hh in range(HEAD_PAIR):
            qb = qb_sc[hh]
            gmax = None
            for k in range(group):
                s = jnp.dot(kb_sc[hh, pl.ds(k0 + k * blk, blk), :], qb, preferred_element_type=F32)
                s_sc[hh, k * blk:(k + 1) * blk, :] = s
                smax = jnp.max(s, axis=0, keepdims=True)
                gmax = smax if gmax is None else jnp.maximum(gmax, smax)
            gmax_sc[hh] = gmax

    def softmax_stage(g, buf):
        s_sc, gmax_sc = buf
        for hh in range(HEAD_PAIR):
            m_old = m_sc[hh]
            m_new = jnp.maximum(m_old, gmax_sc[hh])
            alpha = jnp.exp2(m_old - m_new)
            for k in range(group):
                e_sc[hh, k * blk:(k + 1) * blk, :] = jnp.exp2(
                    s_sc[hh, k * blk:(k + 1) * blk, :] - m_new).astype(BF16)
            pv = jnp.dot(vtg_sc[hh, g], e_sc[hh], preferred_element_type=F32)
            m_sc[hh] = m_new
            acc_sc[hh] = alpha * acc_sc[hh] + pv

    n_groups = lax.div(n + (group - 1), group)

    score_stage(0, bufs[0])

    def pipelined(g, carry):
        for parity in range(2):
            @pl.when(jnp.bitwise_and(g, 1) == parity)
            def _():
                score_stage(g + 1, bufs[1 - parity])
                softmax_stage(g, bufs[parity])
        return carry

    lax.fori_loop(0, n_groups - 1, pipelined, 0)

    last = n_groups - 1
    for parity in range(2):
        @pl.when((n_groups > 0) & (jnp.bitwise_and(last, 1) == parity))
        def _():
            softmax_stage(last, bufs[parity])

    out_t = jnp.zeros((LANES, blk), F32)
    for hh in range(HEAD_PAIR):
        in_head = (drow >= hh * HEAD_DIM) & (drow < (hh + 1) * HEAD_DIM)
        a0 = aux0[hh]
        out_t = jnp.where(in_head, acc_sc[hh] / acc_sc[hh, a0:a0 + 1, :], out_t)
    o_ref[...] = out_t.T


def attn_call(y2, *, bsz, seq):
    blk = MOBA_BLOCK
    nb = seq // blk
    width = ATT_HEADS * HEAD_DIM
    pairs = width // LANES
    group = math.gcd(nb, ATT_KEY_GROUP)
    assert nb % AUX_PEN_SLOTS == 0 and AUX_PEN_SLOTS % group == 0 and HEAD_PAIR == 2
    kern = functools.partial(_attn_kernel, nb=nb, group=group)
    return pl.pallas_call(
        kern,
        grid=(bsz, pairs, nb),
        in_specs=[pl.BlockSpec((blk, LANES), lambda b, p, n: (b * nb + n, p)),
                  pl.BlockSpec((seq, LANES), lambda b, p, n: (b, pairs + p)),
                  pl.BlockSpec((seq, LANES), lambda b, p, n: (b, 2 * pairs + p))],
        out_specs=pl.BlockSpec((blk, LANES), lambda b, p, n: (b * nb + n, p)),
        out_shape=jax.ShapeDtypeStruct((bsz * seq, width), F32),
        scratch_shapes=[pltpu.VMEM((HEAD_PAIR, nb * blk, LANES), BF16),
                        pltpu.VMEM((HEAD_PAIR, nb // group, LANES, group * blk), BF16),
                        pltpu.VMEM((HEAD_PAIR, nb, LANES, blk), BF16),
                        pltpu.VMEM((nb, LANES), F32),
                        pltpu.VMEM((HEAD_PAIR * nb, LANES), F32),
                        pltpu.VMEM((HEAD_PAIR, AUX_PEN0, blk), F32),
                        pltpu.VMEM((HEAD_PAIR, AUX_PEN0, blk), F32),
                        pltpu.VMEM((HEAD_PAIR, AUX_PEN0, blk), F32),
                        pltpu.VMEM((HEAD_PAIR, LANES, blk), BF16),
                        pltpu.VMEM((HEAD_PAIR, nb, blk), F32),
                        pltpu.VMEM((HEAD_PAIR, 1, blk), F32),
                        pltpu.VMEM((HEAD_PAIR, LANES, blk), F32),
                        pltpu.VMEM((HEAD_PAIR, group * blk, blk), F32),
                        pltpu.VMEM((HEAD_PAIR, group * blk, blk), F32),
                        pltpu.VMEM((HEAD_PAIR, 1, blk), F32),
                        pltpu.VMEM((HEAD_PAIR, 1, blk), F32),
                        pltpu.VMEM((HEAD_PAIR, group * blk, blk), BF16)],
        compiler_params=_params(("arbitrary", "arbitrary", "arbitrary")),
        name="moba_attn",
    )(y2, y2, y2)


def _s5_param_kernel(lr_ref, li_ref, ldt_ref, bre_ref, bim_ref, w_ref, ar_ref, ai_ref, *, rows_per_half):
    lr = lr_ref[...]
    li = li_ref[...]
    dt = jnp.exp(ldt_ref[...])
    mag = jnp.exp(lr * dt)
    ang = li * dt
    ab_re = mag * jnp.cos(ang)
    ab_im = mag * jnp.sin(ang)
    nr = ab_re - 1.0
    ni = ab_im
    den = lr * lr + li * li
    f_re = (nr * lr + ni * li) / den
    f_im = (ni * lr - nr * li) / den
    half = lr.shape[1]
    kc = bre_ref.shape[0] // 2
    for gh in range(2):
        fr = f_re[gh:gh + 1, :]
        fi = f_im[gh:gh + 1, :]
        bre = bre_ref[gh * kc:(gh + 1) * kc, :]
        bim = bim_ref[gh * kc:(gh + 1) * kc, :]
        w_ref[gh * kc:(gh + 1) * kc, 0:half] = (fr * bre - fi * bim).astype(BF16)
        w_ref[gh * kc:(gh + 1) * kc, half:2 * half] = (fr * bim + fi * bre).astype(BF16)
        ar_ref[gh * rows_per_half:(gh + 1) * rows_per_half, :] = jnp.broadcast_to(
            ab_re[gh:gh + 1, :], (rows_per_half, half))
        ai_ref[gh * rows_per_half:(gh + 1) * rows_per_half, :] = jnp.broadcast_to(
            ab_im[gh:gh + 1, :], (rows_per_half, half))


def s5_param_call(lr2, li2, ldt2, bre_st, bim_st, *, bsz):
    half = lr2.shape[1]
    width = bre_st.shape[0]
    kern = functools.partial(_s5_param_kernel, rows_per_half=bsz)
    return pl.pallas_call(
        kern,
        out_shape=(jax.ShapeDtypeStruct((width, 2 * half), BF16),
                   jax.ShapeDtypeStruct((2 * bsz, half), F32),
                   jax.ShapeDtypeStruct((2 * bsz, half), F32)),
        compiler_params=pltpu.CompilerParams(vmem_limit_bytes=VMEM_LIMIT),
        name="s5_params",
    )(lr2, li2, ldt2, bre_st, bim_st)


def _s5_kernel(u_ref, w_ref, ar_ref, ai_ref, cre_ref, cim_ref, d_ref, wg_ref, bg_ref, o_ref,
               xs_sc, st_sc, *, bsz, tt):
    rows = 2 * bsz
    n_r = rows * tt
    width = u_ref.shape[2]
    kc = width // 2
    half = ar_ref.shape[1]
    t = pl.program_id(0)

    @pl.when(t == 0)
    def _():
        st_sc[...] = jnp.zeros_like(st_sc)

    lane = lax.broadcasted_iota(jnp.int32, (tt, width), 1)
    parts = []
    for gh in range(2):
        keep = (lane >= gh * kc) & (lane < (gh + 1) * kc)
        for b in range(bsz):
            parts.append(jnp.where(keep, u_ref[b], 0.0).astype(BF16))
    uz = jnp.concatenate(parts, axis=0)
    lt = tt.bit_length() - 1
    lr_ = rows.bit_length() - 1
    ro = lax.broadcasted_iota(jnp.int32, (n_r, n_r), 0)
    co = lax.broadcasted_iota(jnp.int32, (n_r, n_r), 1)
    src = lax.shift_left(jnp.bitwise_and(ro, rows - 1), lt) + lax.shift_right_logical(ro, lr_)
    perm = jnp.where(co == src, 1.0, 0.0).astype(BF16)
    up = jnp.dot(perm, uz, preferred_element_type=F32).astype(BF16)
    xs_sc[...] = jnp.dot(up, w_ref[...], preferred_element_type=F32)

    ar = ar_ref[...]
    ai = ai_ref[...]

    def step(i, carry):
        xr, xi = carry
        r0 = pl.multiple_of(i * rows, rows)
        br = xs_sc[pl.ds(r0, rows), 0:half]
        bi = xs_sc[pl.ds(r0, rows), half:2 * half]
        nr = ar * xr - ai * xi + br
        ni = ar * xi + ai * xr + bi
        xs_sc[pl.ds(r0, rows), 0:half] = nr
        xs_sc[pl.ds(r0, rows), half:2 * half] = ni
        return nr, ni

    xr, xi = lax.fori_loop(0, tt, step, (st_sc[:, 0:half], st_sc[:, half:2 * half]))
    st_sc[:, 0:half] = xr
    st_sc[:, half:2 * half] = xi

    y_all = (jnp.dot(xs_sc[:, 0:half].astype(BF16), cre_ref[...], preferred_element_type=F32)
             - jnp.dot(xs_sc[:, half:2 * half].astype(BF16), cim_ref[...], preferred_element_type=F32))
    rr = lax.broadcasted_iota(jnp.int32, (n_r, width), 0)
    ll = lax.broadcasted_iota(jnp.int32, (n_r, width), 1)
    gh_row = lax.shift_right_logical(jnp.bitwise_and(rr, rows - 1), lr_ - 1)
    valid = lax.shift_right_logical(ll, kc.bit_length() - 1) == gh_row
    yz = jnp.where(valid, y_all, 0.0).astype(BF16)
    ro2 = lax.broadcasted_iota(jnp.int32, (bsz * tt, n_r), 0)
    co2 = lax.broadcasted_iota(jnp.int32, (bsz * tt, n_r), 1)
    same_t = lax.shift_right_logical(co2, lr_) == jnp.bitwise_and(ro2, tt - 1)
    same_b = jnp.bitwise_and(co2, bsz - 1) == lax.shift_right_logical(ro2, lt)
    unperm = jnp.where(same_t & same_b, 1.0, 0.0).astype(BF16)
    ypre = jnp.dot(unperm, yz, preferred_element_type=F32)
    for b in range(bsz):
        y = ypre[b * tt:(b + 1) * tt, :] + d_ref[...] * u_ref[b]
        y = jax.nn.gelu(y)
        z = jnp.dot(y.astype(BF16), wg_ref[...], preferred_element_type=F32) + bg_ref[...]
        o_ref[b] = y * jax.nn.sigmoid(z)


def s5_call(y3, w_bu, ar, ai, cre_st, cim_st, d, wg, bg, *, tt=64):
    bsz, seq, _ = y3.shape
    width = d.shape[0]
    half = ar.shape[1]
    assert bsz & (bsz - 1) == 0 and tt & (tt - 1) == 0
    kern = functools.partial(_s5_kernel, bsz=bsz, tt=tt)
    const = lambda t: (0, 0)
    return pl.pallas_call(
        kern,
        grid=(seq // tt,),
        in_specs=[pl.BlockSpec((bsz, tt, width), lambda t: (0, t, 3)),
                  pl.BlockSpec(w_bu.shape, const),
                  pl.BlockSpec(ar.shape, const),
                  pl.BlockSpec(ai.shape, const),
                  pl.BlockSpec(cre_st.shape, const),
                  pl.BlockSpec(cim_st.shape, const),
                  pl.BlockSpec((1, width), const),
                  pl.BlockSpec(wg.shape, const),
                  pl.BlockSpec((1, width), const)],
        out_specs=pl.BlockSpec((bsz, tt, width), lambda t: (0, t, 0)),
        out_shape=jax.ShapeDtypeStruct((bsz, seq, width), F32),
        scratch_shapes=[pltpu.VMEM((2 * bsz * tt, 2 * half), F32),
                        pltpu.VMEM((2 * bsz, 2 * half), F32)],
        compiler_params=_params(("arbitrary",)),
        name="s5_scan",
    )(y3, w_bu, ar, ai, cre_st, cim_st, d.reshape(1, width), wg, bg.reshape(1, width))


def _merge_kernel(x_ref, ya_ref, ys_ref, ga_ref, gs_ref, ada_ref, wa_ref, ws_ref, wo_ref, o_ref):
    a = jnp.dot(ya_ref[...].astype(BF16), wa_ref[...], preferred_element_type=F32)
    s = jnp.dot(ys_ref[...].astype(BF16), ws_ref[...], preferred_element_type=F32)
    merged = jax.nn.sigmoid(ga_ref[...]) * a + jax.nn.sigmoid(gs_ref[...]) * s
    o_ref[...] = x_ref[...] + ada_ref[5:6, :] * jnp.dot(merged.astype(BF16), wo_ref[...],
                                                         preferred_element_type=F32)


def merge_call(x2, y_att, y_ssm, y2, ada3, wa, ws, wo, *, seq, tm=512):
    t, d = x2.shape
    width = y_att.shape[1]
    gcol = (y2.shape[1] - 2 * d) // d
    const = lambda i: (0, 0)
    return pl.pallas_call(
        _merge_kernel,
        grid=(t // tm,),
        in_specs=[pl.BlockSpec((tm, d), lambda i: (i, 0)),
                  pl.BlockSpec((tm, width), lambda i: (i, 0)),
                  pl.BlockSpec((tm, width), lambda i: (i, 0)),
                  pl.BlockSpec((tm, d), lambda i: (i, gcol)),
                  pl.BlockSpec((tm, d), lambda i: (i, gcol + 1)),
                  pl.BlockSpec((None, N_ADA, d), lambda i: ((i * tm) // seq, 0, 0)),
                  pl.BlockSpec(wa.shape, const),
                  pl.BlockSpec(ws.shape, const),
                  pl.BlockSpec(wo.shape, const)],
        out_specs=pl.BlockSpec((tm, d), lambda i: (i, 0)),
        out_shape=jax.ShapeDtypeStruct((t, d), F32),
        compiler_params=_params(("arbitrary",)),
        name="merge",
    )(x2, y_att, y_ssm, y2, y2, ada3, wa, ws, wo)


def _stack_block_diag(a):
    g, r, c = a.shape
    gl = g // 2
    eye = jnp.eye(gl, dtype=a.dtype)
    bd = a.reshape(2, gl, r, c)[:, :, :, None, :] * eye[None, :, None, :, None]
    return bd.reshape(2, gl * r, gl * c)


def s5_mixer(y3, lam_re, lam_im, log_dt, b_re, b_im, c_re, c_im, d_skip, w_glu, b_glu):
    bsz = y3.shape[0]
    g, p = lam_re.shape
    half = (g // 2) * p
    lr2 = lam_re.reshape(2, half)
    li2 = lam_im.reshape(2, half)
    ldt2 = jnp.broadcast_to(log_dt[:, None], (g, p)).reshape(2, half)
    bre_st = _stack_block_diag(jnp.swapaxes(b_re, 1, 2)).reshape(-1, half)
    bim_st = _stack_block_diag(jnp.swapaxes(b_im, 1, 2)).reshape(-1, half)
    w_bu, ar, ai = s5_param_call(lr2, li2, ldt2, bre_st, bim_st, bsz=bsz)
    cre_st = jnp.concatenate(list(_stack_block_diag(jnp.swapaxes(c_re, 1, 2))), axis=1).astype(BF16)
    cim_st = jnp.concatenate(list(_stack_block_diag(jnp.swapaxes(c_im, 1, 2))), axis=1).astype(BF16)
    return s5_call(y3, w_bu, ar, ai, cre_st, cim_st, d_skip, w_glu.astype(BF16), b_glu)


def kernel(x, c, w_ada, b_ada, norm_ffn1, w_ffn1_in, w_ffn1_out, norm_mix, w_in, lam_re, lam_im,
           log_dt, ssm_b_re, ssm_b_im, ssm_c_re, ssm_c_im, ssm_d, w_glu, b_glu, w_br_att, w_br_ssm,
           w_out, norm_ffn2, w_ffn2_in, w_ffn2_out, norm_final):
    bsz, seq, d = x.shape
    depth = w_ada.shape[0]
    x2 = x.reshape(bsz * seq, d)
    for l in range(depth):
        last = l == depth - 1
        ada3 = ada_call(c, w_ada[l], b_ada[l]).reshape(bsz, N_ADA, d)
        x2 = ffn_call(x2, ada3, norm_ffn1[l], w_ffn1_in[l].astype(BF16), w_ffn1_out[l].astype(BF16),
                      norm_final, seq=seq, row0=0, final_norm=False)
        y2 = inproj_call(x2, ada3, norm_mix[l], w_in[l].astype(BF16), seq=seq)
        y_att = attn_call(y2, bsz=bsz, seq=seq)
        y_ssm = s5_mixer(y2.reshape(bsz, seq, -1), lam_re[l], lam_im[l], log_dt[l], ssm_b_re[l],
                         ssm_b_im[l], ssm_c_re[l], ssm_c_im[l], ssm_d[l], w_glu[l], b_glu[l])
        x2 = merge_call(x2, y_att, y_ssm.reshape(bsz * seq, -1), y2, ada3, w_br_att[l].astype(BF16),
                        w_br_ssm[l].astype(BF16), w_out[l].astype(BF16), seq=seq)
        x2 = ffn_call(x2, ada3, norm_ffn2[l], w_ffn2_in[l].astype(BF16), w_ffn2_out[l].astype(BF16),
                      norm_final, seq=seq, row0=6, final_norm=last)
    if depth == 0:
        raise ValueError("depth must be positive")
    return x2.reshape(bsz, seq, d)
```
